```python
import math
import jax, jax.numpy as jnp
from jax import lax
import numpy as np

D_MODEL = 1024
BATCH = 8
SEQ = 2048
DEPTH = 1

MIX_WIDTH = D_MODEL
CONV_WIDTH = D_MODEL // 2
ATTN_WIDTH = MIX_WIDTH - CONV_WIDTH
N_DIFF_HEADS = 4
DIFF_HEAD_DIM = ATTN_WIDTH // (2 * N_DIFF_HEADS)
DIFF_V_DIM = 2 * DIFF_HEAD_DIM
CONV_KERNEL = 31
CONV_PAD = (CONV_KERNEL - 1) // 2
ROPE_THETA = 10000.0
Q_BLOCK = 128
EPS = 1e-6
LN_EPS = 1e-5
IN_COLS = 3 * CONV_WIDTH + 4 * ATTN_WIDTH

kernel_name = "hybrid_conformer_conv_diff_attn_parallel"


def lambda_init_fn(layer_idx):
    return 0.8 - 0.6 * math.exp(-0.3 * layer_idx)


def rmsnorm(x, g, eps=EPS):
    xf = x.astype(jnp.float32)
    y = xf * lax.rsqrt(jnp.mean(xf * xf, axis=-1, keepdims=True) + eps)
    return (y * g.astype(jnp.float32)).astype(x.dtype)


def layernorm(x, g, b, eps=LN_EPS):
    xf = x.astype(jnp.float32)
    mu = jnp.mean(xf, axis=-1, keepdims=True)
    var = jnp.mean(jnp.square(xf - mu), axis=-1, keepdims=True)
    y = (xf - mu) * lax.rsqrt(var + eps)
    return (y * g.astype(jnp.float32) + b.astype(jnp.float32)).astype(x.dtype)


def rope(t, seq_len):
    dh = t.shape[-1]
    half = dh // 2
    pos = jnp.arange(seq_len, dtype=jnp.float32)
    inv_freq = 1.0 / (ROPE_THETA ** (jnp.arange(half, dtype=jnp.float32) * 2.0 / dh))
    ang = pos[:, None] * inv_freq[None, :]
    ang = jnp.concatenate([ang, ang], axis=-1)
    cos = jnp.cos(ang)[None, :, None, :].astype(t.dtype)
    sin = jnp.sin(ang)[None, :, None, :].astype(t.dtype)
    t1, t2 = t[..., :half], t[..., half:]
    rot = jnp.concatenate([-t2, t1], axis=-1)
    return t * cos + rot * sin


def conformer_conv_branch(a_val, a_glu, conv_w, conv_b, ln_g, ln_b, w_pw, b_pw):
    u = a_val * jax.nn.sigmoid(a_glu)
    u = lax.conv_general_dilated(
        u, conv_w[:, None, :], window_strides=(1,),
        padding=[(CONV_PAD, CONV_PAD)],
        dimension_numbers=("NWC", "WIO", "NWC"),
        feature_group_count=CONV_WIDTH) + conv_b
    u = layernorm(u, ln_g, ln_b)
    u = jax.nn.silu(u)
    return jnp.einsum("bsc,ce->bse", u, w_pw) + b_pw


def diff_attention_branch(q, k, v, lq1, lk1, lq2, lk2, head_g, lam_init):
    b, s, _ = q.shape
    q = rope(q.reshape(b, s, 2 * N_DIFF_HEADS, DIFF_HEAD_DIM), s)
    k = rope(k.reshape(b, s, 2 * N_DIFF_HEADS, DIFF_HEAD_DIM), s)
    q = q.reshape(b, s, N_DIFF_HEADS, 2, DIFF_HEAD_DIM).transpose(3, 0, 2, 1, 4)
    k = k.reshape(b, s, N_DIFF_HEADS, 2, DIFF_HEAD_DIM).transpose(3, 0, 2, 1, 4)
    v = v.reshape(b, s, N_DIFF_HEADS, DIFF_V_DIM).transpose(0, 2, 1, 3)
    k1, k2 = k[0], k[1]
    lam = (jnp.exp(jnp.sum(lq1.astype(jnp.float32) * lk1.astype(jnp.float32)))
           - jnp.exp(jnp.sum(lq2.astype(jnp.float32) * lk2.astype(jnp.float32)))
           + lam_init)
    scale = DIFF_HEAD_DIM ** -0.5
    nb = s // Q_BLOCK
    qb = q.reshape(2, b, N_DIFF_HEADS, nb, Q_BLOCK, DIFF_HEAD_DIM).transpose(3, 0, 1, 2, 4, 5)

    def block(qblk):
        s1 = jnp.einsum("bhqd,bhkd->bhqk", qblk[0], k1).astype(jnp.float32) * scale
        s2 = jnp.einsum("bhqd,bhkd->bhqk", qblk[1], k2).astype(jnp.float32) * scale
        w = jax.nn.softmax(s1, axis=-1) - lam * jax.nn.softmax(s2, axis=-1)
        return jnp.einsum("bhqk,bhke->bhqe", w.astype(v.dtype), v)

    o = lax.map(block, qb)
    o = o.transpose(1, 3, 0, 2, 4).reshape(b, s, N_DIFF_HEADS, DIFF_V_DIM)
    o = rmsnorm(o, head_g) * jnp.asarray(1.0 - lam_init, dtype=o.dtype)
    return o.reshape(b, s, ATTN_WIDTH)


def setup_inputs(seed: int = 0) -> dict:
    key = jax.random.key(seed)
    ks = jax.random.split(key, 18)
    f32 = jnp.float32
    nrm = lambda k, shape, sc: jax.random.normal(k, shape, f32) * sc
    return {
        "x": nrm(ks[0], (BATCH, SEQ, D_MODEL), 1.0),
        "norm_g": 1.0 + nrm(ks[1], (DEPTH, D_MODEL), 0.02),
        "w_in": nrm(ks[2], (DEPTH, D_MODEL, IN_COLS), D_MODEL ** -0.5),
        "conv_w": nrm(ks[3], (DEPTH, CONV_KERNEL, CONV_WIDTH), CONV_KERNEL ** -0.5),
        "conv_b": nrm(ks[4], (DEPTH, CONV_WIDTH), 0.02),
        "conv_ln_g": 1.0 + nrm(ks[5], (DEPTH, CONV_WIDTH), 0.02),
        "conv_ln_b": nrm(ks[6], (DEPTH, CONV_WIDTH), 0.02),
        "w_pw": nrm(ks[7], (DEPTH, CONV_WIDTH, CONV_WIDTH), CONV_WIDTH ** -0.5),
        "b_pw": nrm(ks[8], (DEPTH, CONV_WIDTH), 0.02),
        "lambda_q1": nrm(ks[9], (DEPTH, DIFF_HEAD_DIM), 0.1),
        "lambda_k1": nrm(ks[10], (DEPTH, DIFF_HEAD_DIM), 0.1),
        "lambda_q2": nrm(ks[11], (DEPTH, DIFF_HEAD_DIM), 0.1),
        "lambda_k2": nrm(ks[12], (DEPTH, DIFF_HEAD_DIM), 0.1),
        "head_norm_g": 1.0 + nrm(ks[13], (DEPTH, DIFF_V_DIM), 0.02),
        "w_out": nrm(ks[14], (DEPTH, MIX_WIDTH, D_MODEL), MIX_WIDTH ** -0.5),
        "final_norm_g": 1.0 + nrm(ks[15], (D_MODEL,), 0.02),
    }


def reference(x, norm_g, w_in, conv_w, conv_b, conv_ln_g, conv_ln_b, w_pw, b_pw,
              lambda_q1, lambda_k1, lambda_q2, lambda_k2, head_norm_g, w_out,
              final_norm_g):
    C, A = CONV_WIDTH, ATTN_WIDTH
    for l in range(DEPTH):
        h = rmsnorm(x, norm_g[l])
        p = jnp.einsum("bsd,de->bse", h, w_in[l])
        a_val = p[..., 0:C]
        a_glu = p[..., C:2 * C]
        a_gate = p[..., 2 * C:3 * C]
        o = 3 * C
        q = p[..., o:o + A]
        k = p[..., o + A:o + 2 * A]
        v = p[..., o + 2 * A:o + 3 * A]
        b_gate = p[..., o + 3 * A:o + 4 * A]
        y_a = conformer_conv_branch(a_val, a_glu, conv_w[l], conv_b[l], conv_ln_g[l],
                                    conv_ln_b[l], w_pw[l], b_pw[l]) * jax.nn.silu(a_gate)
        y_b = diff_attention_branch(q, k, v, lambda_q1[l], lambda_k1[l], lambda_q2[l],
                                    lambda_k2[l], head_norm_g[l], lambda_init_fn(l)) * jax.nn.silu(b_gate)
        y = jnp.concatenate([y_a, y_b], axis=-1)
        x = x + jnp.einsum("bse,ed->bsd", y, w_out[l])
    return rmsnorm(x, final_norm_g)
```

```python
import functools
import math

import jax
import jax.numpy as jnp
from jax import lax
from jax.experimental import pallas as pl
from jax.experimental.pallas import tpu as pltpu

D_MODEL = 1024
CONV_WIDTH = 512
ATTN_WIDTH = 512
N_HEADS = 4
HEAD_DIM = 64
V_DIM = 128
Q_BLOCK = 128
N_QBLK = 16
CONV_KERNEL = 31
CONV_PAD = 15
ROPE_THETA = 10000.0
EPS = 1e-6
LN_EPS = 1e-5
IN_COLS = 3 * CONV_WIDTH + 4 * ATTN_WIDTH
LAMBDA_INIT = 0.8 - 0.6 * math.exp(-0.3 * 0)

LANES = 128
HALO_ROWS = 16
VMEM_LIMIT = 56 * 1024 * 1024

TM_PROJ = 512
TQ_ATTN = 256
TS_OUT = 256

F32 = jnp.float32
BF16 = jnp.bfloat16


def _sigmoid(x):
    return 1.0 / (1.0 + jnp.exp(-x))


def _rope_group(t, cos, sin_signed):
    return t * cos + pltpu.roll(t, 64, axis=1) * sin_signed


def _in_proj_kernel(x_ref, ng_ref, w_ref, cos_ref, sin_ref,
                    u_ref, ga_ref, q_ref, k_ref, v_ref, gb_ref, *, q_scale):
    x = x_ref[...]
    ms = jnp.mean(x * x, axis=-1, keepdims=True)
    h = (x * lax.rsqrt(ms + EPS) * ng_ref[...]).astype(BF16)

    def proj(c0):
        return jnp.dot(h, w_ref[:, c0:c0 + 512], preferred_element_type=F32)

    a_val = proj(0)
    a_glu = proj(512)
    u_ref[...] = a_val * _sigmoid(a_glu)
    a_gate = proj(1024)
    ga_ref[...] = (a_gate * _sigmoid(a_gate)).astype(BF16)

    cos = cos_ref[...]
    sin = sin_ref[...]
    q = proj(1536)
    k = proj(2048)
    for g in range(N_HEADS):
        sl = slice(g * LANES, (g + 1) * LANES)
        q_ref[:, sl] = (_rope_group(q[:, sl], cos, sin) * q_scale).astype(BF16)
        k_ref[:, sl] = _rope_group(k[:, sl], cos, sin).astype(BF16)
    v_ref[...] = proj(2560).astype(BF16)
    b_gate = proj(3072)
    gb_ref[...] = (b_gate * _sigmoid(b_gate)).astype(BF16)


def _in_proj(x2, norm_g, w_bf, cos_t, sin_t, seq):
    rows = x2.shape[0]
    tm = TM_PROJ
    pos_blocks = seq // tm
    row_spec = lambda n: pl.BlockSpec((tm, n), lambda i: (i, 0))
    tab_spec = pl.BlockSpec((tm, LANES), lambda i: (i % pos_blocks, 0))
    out_shapes = (
        jax.ShapeDtypeStruct((rows, CONV_WIDTH), F32),
        jax.ShapeDtypeStruct((rows, CONV_WIDTH), BF16),
        jax.ShapeDtypeStruct((rows, ATTN_WIDTH), BF16),
        jax.ShapeDtypeStruct((rows, ATTN_WIDTH), BF16),
        jax.ShapeDtypeStruct((rows, ATTN_WIDTH), BF16),
        jax.ShapeDtypeStruct((rows, ATTN_WIDTH), BF16),
    )
    q_scale = (HEAD_DIM ** -0.5) * math.log2(math.e)
    return pl.pallas_call(
        functools.partial(_in_proj_kernel, q_scale=q_scale),
        grid=(rows // tm,),
        in_specs=[
            row_spec(D_MODEL),
            pl.BlockSpec((1, D_MODEL), lambda i: (0, 0)),
            pl.BlockSpec((D_MODEL, IN_COLS), lambda i: (0, 0)),
            tab_spec, tab_spec,
        ],
        out_specs=tuple(row_spec(512) for _ in range(6)),
        out_shape=out_shapes,
        compiler_params=pltpu.CompilerParams(
            dimension_semantics=("arbitrary",), vmem_limit_bytes=VMEM_LIMIT),
        name="in_proj",
    )(x2, norm_g, w_bf, cos_t, sin_t)


def _attn_kernel(lam_ref, perm_ref, q_ref, k_ref, v_ref, gb_ref, hg_ref, o_ref):
    tq = o_ref.shape[0]
    lp = lam_ref[...]
    lam = (jnp.exp(jnp.sum(lp[0:1] * lp[1:2], axis=-1, keepdims=True))
           - jnp.exp(jnp.sum(lp[2:3] * lp[3:4], axis=-1, keepdims=True))
           + LAMBDA_INIT)

    q = jnp.dot(perm_ref[...], q_ref[...].reshape(tq, LANES),
                preferred_element_type=F32).astype(BF16)
    lane = lax.broadcasted_iota(jnp.int32, q.shape, 1)
    first = (lane % 64) < 32
    zero = jnp.zeros_like(q)
    qq = jnp.concatenate([jnp.where(first, q, zero), jnp.where(first, zero, q)], axis=0)
    s = lax.dot_general(qq, k_ref[...], (((1,), (1,)), ((), ())),
                        preferred_element_type=F32)
    s1 = s[:tq]
    s2 = s[tq:]
    e1 = jnp.exp2(s1 - jnp.max(s1, axis=-1, keepdims=True))
    e2 = jnp.exp2(s2 - jnp.max(s2, axis=-1, keepdims=True))
    r1 = 1.0 / jnp.sum(e1, axis=-1, keepdims=True)
    r2 = lam / jnp.sum(e2, axis=-1, keepdims=True)
    w = (e1 * r1 - e2 * r2).astype(BF16)
    o = jnp.dot(w, v_ref[...], preferred_element_type=F32)
    o = o * lax.rsqrt(jnp.mean(o * o, axis=-1, keepdims=True) + EPS) * hg_ref[...]
    o = o * (1.0 - LAMBDA_INIT)
    o_ref[...] = (o * gb_ref[...].astype(F32)).astype(BF16)


def _attention(lam_p, q, k, v, gb, head_g):
    b, seq, _ = q.shape
    tq = TQ_ATTN
    jb = tq // N_QBLK
    r = jnp.arange(tq)
    perm = (jnp.arange(tq)[None, :] == ((r % N_QBLK) * jb + r // N_QBLK)[:, None]).astype(BF16)
    q4 = q.reshape(b, N_QBLK, Q_BLOCK, ATTN_WIDTH)
    qspec = pl.BlockSpec((None, N_QBLK, jb, LANES), lambda bi, h, i: (bi, 0, i, h))
    rowspec = pl.BlockSpec((None, tq, LANES), lambda bi, h, i: (bi, i, h))
    kvspec = pl.BlockSpec((None, seq, LANES), lambda bi, h, i: (bi, 0, h))
    return pl.pallas_call(
        _attn_kernel,
        grid=(b, N_HEADS, seq // tq),
        in_specs=[
            pl.BlockSpec((4, HEAD_DIM), lambda bi, h, i: (0, 0)),
            pl.BlockSpec((tq, tq), lambda bi, h, i: (0, 0)),
            qspec, kvspec, kvspec, rowspec,
            pl.BlockSpec((1, V_DIM), lambda bi, h, i: (0, 0)),
        ],
        out_specs=rowspec,
        out_shape=jax.ShapeDtypeStruct((b, seq, ATTN_WIDTH), BF16),
        compiler_params=pltpu.CompilerParams(
            dimension_semantics=("arbitrary", "arbitrary", "arbitrary"),
            vmem_limit_bytes=VMEM_LIMIT),
        name="diff_attn",
    )(lam_p, perm, q4, k, v, gb, head_g)


def _out_kernel(u_ref, up_ref, un_ref, cw_ref, cb_ref, lg_ref, lb_ref, wpw_ref, bpw_ref,
                ga_ref, yb_ref, wo_ref, x_ref, fg_ref, o_ref, buf_ref):
    i = pl.program_id(1)
    n = pl.num_programs(1)
    ts = u_ref.shape[0]
    halo = HALO_ROWS

    prev = jnp.where(i > 0, up_ref[...], 0.0)
    nxt = jnp.where(i < n - 1, un_ref[...], 0.0)
    buf_ref[0:halo, :] = prev
    buf_ref[halo:halo + ts, :] = u_ref[...]
    buf_ref[halo + ts:, :] = nxt

    acc = jnp.zeros((ts, CONV_WIDTH), F32)
    for t in range(CONV_KERNEL):
        off = halo - CONV_PAD + t
        acc = acc + buf_ref[off:off + ts, :] * cw_ref[t:t + 1, :]
    c = acc + cb_ref[...]

    mu = jnp.mean(c, axis=-1, keepdims=True)
    d = c - mu
    var = jnp.mean(d * d, axis=-1, keepdims=True)
    z = d * lax.rsqrt(var + LN_EPS) * lg_ref[...] + lb_ref[...]
    z = (z * _sigmoid(z)).astype(BF16)
    ya = jnp.dot(z, wpw_ref[...], preferred_element_type=F32) + bpw_ref[...]
    ya = (ya * ga_ref[...].astype(F32)).astype(BF16)

    y = jnp.dot(ya, wo_ref[0:CONV_WIDTH, :], preferred_element_type=F32)
    y = y + jnp.dot(yb_ref[...], wo_ref[CONV_WIDTH:, :], preferred_element_type=F32)
    r = x_ref[...] + y
    o_ref[...] = r * lax.rsqrt(jnp.mean(r * r, axis=-1, keepdims=True) + EPS) * fg_ref[...]


def _out_stage(u, conv_w, conv_b, ln_g, ln_b, wpw_bf, b_pw, ga, yb, wo_bf, x, final_g):
    b, seq, _ = x.shape
    ts = TS_OUT
    hb = ts // HALO_ROWS
    n_hblk = seq // HALO_ROWS
    row = lambda n: pl.BlockSpec((None, ts, n), lambda bi, i: (bi, i, 0))
    full = lambda r, c: pl.BlockSpec((r, c), lambda bi, i: (0, 0))
    prev_spec = pl.BlockSpec((None, HALO_ROWS, CONV_WIDTH),
                             lambda bi, i: (bi, jnp.maximum(i * hb - 1, 0), 0))
    next_spec = pl.BlockSpec((None, HALO_ROWS, CONV_WIDTH),
                             lambda bi, i: (bi, jnp.minimum((i + 1) * hb, n_hblk - 1), 0))
    return pl.pallas_call(
        _out_kernel,
        grid=(b, seq // ts),
        in_specs=[
            row(CONV_WIDTH), prev_spec, next_spec,
            full(CONV_KERNEL, CONV_WIDTH), full(1, CONV_WIDTH), full(1, CONV_WIDTH),
            full(1, CONV_WIDTH), full(CONV_WIDTH, CONV_WIDTH), full(1, CONV_WIDTH),
            row(CONV_WIDTH), row(ATTN_WIDTH), full(D_MODEL, D_MODEL), row(D_MODEL),
            full(1, D_MODEL),
        ],
        out_specs=row(D_MODEL),
        out_shape=jax.ShapeDtypeStruct((b, seq, D_MODEL), F32),
        scratch_shapes=[pltpu.VMEM((ts + 2 * HALO_ROWS, CONV_WIDTH), F32)],
        compiler_params=pltpu.CompilerParams(
            dimension_semantics=("arbitrary", "arbitrary"), vmem_limit_bytes=VMEM_LIMIT),
        name="conv_out",
    )(u, u, u, conv_w, conv_b, ln_g, ln_b, wpw_bf, b_pw, ga, yb, wo_bf, x, final_g)


def _permute_qk_cols(w):
    d = w.shape[0]
    return w.reshape(d, N_HEADS, 2, 2, 32).transpose(0, 1, 3, 2, 4).reshape(d, ATTN_WIDTH)


def _rope_tables(seq):
    half = HEAD_DIM // 2
    pos = jnp.arange(seq, dtype=F32)
    inv_freq = 1.0 / (ROPE_THETA ** (jnp.arange(half, dtype=F32) * 2.0 / HEAD_DIM))
    ang = pos[:, None] * inv_freq[None, :]
    cos = jnp.cos(ang)
    sin = jnp.sin(ang)
    return (jnp.concatenate([cos, cos, cos, cos], axis=-1),
            jnp.concatenate([-sin, -sin, sin, sin], axis=-1))


def kernel(x, norm_g, w_in, conv_w, conv_b, conv_ln_g, conv_ln_b, w_pw, b_pw, lambda_q1, lambda_k1, lambda_q2, lambda_k2, head_norm_g, w_out, final_norm_g):
    b, seq, d = x.shape
    assert (b, seq, d) == (8, 2048, D_MODEL) and norm_g.shape[0] == 1
    c, a = CONV_WIDTH, ATTN_WIDTH
    o = 3 * c
    w = w_in[0]
    w_bf = jnp.concatenate(
        [w[:, :o], _permute_qk_cols(w[:, o:o + a]), _permute_qk_cols(w[:, o + a:o + 2 * a]),
         w[:, o + 2 * a:]], axis=1).astype(BF16)
    cos_t, sin_t = _rope_tables(seq)

    u, ga, q, k, v, gb = _in_proj(x.reshape(b * seq, d), norm_g, w_bf, cos_t, sin_t, seq)

    lam_p = jnp.concatenate([lambda_q1, lambda_k1, lambda_q2, lambda_k2], axis=0)
    shp = (b, seq, a)
    yb = _attention(lam_p, q.reshape(shp), k.reshape(shp), v.reshape(shp), gb.reshape(shp),
                    head_norm_g)

    return _out_stage(u.reshape(b, seq, c), conv_w[0], conv_b, conv_ln_g, conv_ln_b,
                      w_pw[0].astype(BF16), b_pw, ga.reshape(b, seq, c), yb,
                      w_out[0].astype(BF16), x, final_norm_g.reshape(1, d))
```
